```python
import jax
import jax.numpy as jnp
from jax import lax
import numpy as np

D_MODEL = 2048
BATCH = 2
SEQ = 8192
DEPTH = 4

GRID_W = 64
CTX_LEN = 256
N_MOD = 6
NORM_EPS = 1e-6
NEG_INF = -1e30

A_HEADS = 8
A_KV_HEADS = 2
A_GROUP = A_HEADS // A_KV_HEADS
A_HEAD_DIM = 128
A_Q = A_HEADS * A_HEAD_DIM
A_KV = A_KV_HEADS * A_HEAD_DIM
ROPE_THETA = 10000.0
Q_BLOCK = 128

B_HEADS = 16
B_HEAD_DIM = 64
B_WIDTH = B_HEADS * B_HEAD_DIM
DECAY_LORA = 64
ICLR_LORA = 64
GATE_LORA = 160
B_SHIFT = 3 * B_WIDTH + DECAY_LORA + ICLR_LORA + GATE_LORA
GN_EPS = 64e-5

EVEN_IN = A_Q + 2 * A_KV + B_SHIFT
EVEN_MIX = A_Q + B_WIDTH

C_HEADS = 16
C_HEAD_DIM = 128
C_WIDTH = C_HEADS * C_HEAD_DIM
NA_ROWS = 8
NA_COLS = 16

N_EXPERTS = 16
N_GROUPS = 4
EXPERTS_PER_GROUP = N_EXPERTS // N_GROUPS
TOP_K = 2
D_EXPERT = 1024

N_EVEN = (DEPTH + 1) // 2
N_ODD = DEPTH // 2

kernel_name = 'hybrid_dit_gqa_rwkv7_natten_groupmoe'


def rms_norm(x, g):
    xf = x.astype(jnp.float32)
    y = xf * lax.rsqrt(jnp.mean(xf * xf, axis=-1, keepdims=True) + NORM_EPS)
    return (y * g.astype(jnp.float32)).astype(x.dtype)


def split_cols(h, sizes):
    return jnp.split(h, np.cumsum(sizes)[:-1].tolist(), axis=-1)


def split_heads(t, n_heads):
    return t.reshape(*t.shape[:-1], n_heads, t.shape[-1] // n_heads)


def axial_rope_tables(n_tokens, head_dim):
    t = jnp.arange(n_tokens)
    row = (t // GRID_W).astype(jnp.float32)
    col = (t % GRID_W).astype(jnp.float32)
    n_freq = head_dim // 4
    inv = ROPE_THETA ** (-jnp.arange(n_freq, dtype=jnp.float32) / n_freq)
    ang = jnp.stack([row[:, None] * inv, col[:, None] * inv], axis=1)
    return jnp.cos(ang)[:, None], jnp.sin(ang)[:, None]


def apply_axial_rope(x, cos, sin):
    B_, L, H, hd = x.shape
    xr = x.astype(jnp.float32).reshape(B_, L, H, 2, 2, hd // 4)
    x1, x2 = xr[..., 0, :], xr[..., 1, :]
    y = jnp.stack([x1 * cos - x2 * sin, x2 * cos + x1 * sin], axis=-2)
    return y.reshape(B_, L, H, hd).astype(x.dtype)


def gqa_attend(q, k, v):
    s = jnp.einsum('bqkgd,bskd->bkgqs', q, k).astype(jnp.float32) * (q.shape[-1] ** -0.5)
    p = jax.nn.softmax(s, axis=-1).astype(v.dtype)
    return jnp.einsum('bkgqs,bskd->bqkgd', p, v)


def blocked_gqa(q, k, v):
    B_, L = q.shape[:2]
    nb = L // Q_BLOCK
    qb = jnp.moveaxis(q.reshape(B_, nb, Q_BLOCK, *q.shape[2:]), 1, 0)
    o = lax.map(lambda qq: gqa_attend(qq, k, v), qb)
    return jnp.moveaxis(o, 0, 1).reshape(B_, L, -1)


def centred_shift(f, mu_prev, mu_next):
    pad = jnp.zeros_like(f[:, :1])
    prev = jnp.concatenate([pad, f[:, :-1]], axis=1)
    nxt = jnp.concatenate([f[:, 1:], pad], axis=1)
    return f + mu_prev * (prev - f) + mu_next * (nxt - f)


def rwkv7_features(f, mu, w0, w_up, a0, a_up, g_up, k_k, k_a):
    f32 = jnp.float32
    f = centred_shift(f.astype(f32), mu[0].astype(f32), mu[1].astype(f32))
    r, k, v, xw, xa, xg = split_cols(f, [B_WIDTH, B_WIDTH, B_WIDTH, DECAY_LORA, ICLR_LORA, GATE_LORA])
    g = jax.nn.sigmoid(xg) @ g_up.astype(f32)
    kk = split_heads(k * k_k.astype(f32), B_HEADS)
    kk = kk * lax.rsqrt(jnp.sum(kk * kk, axis=-1, keepdims=True) + 1e-12)
    per_dir = []
    for d in range(2):
        w_log = -jax.nn.softplus(-(w0[d].astype(f32) + jnp.tanh(xw) @ w_up[d].astype(f32))) - 0.5
        decay = jnp.exp(-jnp.exp(w_log))
        a = jax.nn.sigmoid(a0[d].astype(f32) + xa @ a_up[d].astype(f32))
        k_d = k * (1.0 + (a - 1.0) * k_a.astype(f32))
        per_dir.append((split_heads(decay, B_HEADS), split_heads(k_d, B_HEADS), split_heads(a, B_HEADS)))
    return split_heads(r, B_HEADS), split_heads(v, B_HEADS), kk, g, per_dir


def rwkv7_scan(s0, r, decay, k, v, kk, a, reverse):
    def step(s, inp):
        r_t, w_t, k_t, v_t, kk_t, a_t = inp
        sa = jnp.einsum('bhvk,bhk->bhv', s, -kk_t)
        s = (s * w_t[:, :, None, :] + sa[..., None] * (kk_t * a_t)[:, :, None, :]
             + v_t[..., None] * k_t[:, :, None, :])
        return s, jnp.einsum('bhvk,bhk->bhv', s, r_t)
    xs = tuple(jnp.moveaxis(t, 1, 0) for t in (r, decay, k, v, kk, a))
    s_final, y = lax.scan(step, s0, xs, reverse=reverse)
    return s_final, jnp.moveaxis(y, 0, 1)


def rwkv7_output(y, bonus, g, gn_g, gn_b):
    mean = jnp.mean(y, axis=-1, keepdims=True)
    var = jnp.mean(jnp.square(y - mean), axis=-1, keepdims=True)
    y = (y - mean) * lax.rsqrt(var + GN_EPS)
    y = y.reshape(*y.shape[:2], B_WIDTH) * gn_g.astype(jnp.float32) + gn_b.astype(jnp.float32)
    return (y + bonus.reshape(*bonus.shape[:2], B_WIDTH)) * g


def rwkv7_bidirectional(f_lat, f_ctx, mu, w0, w_up, a0, a_up, g_up, k_k, k_a, r_k, gn_g, gn_b, need_ctx):
    prm = (mu, w0, w_up, a0, a_up, g_up, k_k, k_a)
    r_l, v_l, kk_l, g_l, dir_l = rwkv7_features(f_lat, *prm)
    r_c, v_c, kk_c, g_c, dir_c = rwkv7_features(f_ctx, *prm)
    B_ = f_lat.shape[0]
    r_k = r_k.astype(jnp.float32)
    ys_l, ys_c, bon_l, bon_c = [], [], [], []
    for d in range(2):
        rev = d == 1
        s0 = jnp.zeros((B_, B_HEADS, B_HEAD_DIM, B_HEAD_DIM), jnp.float32)
        w_c, k_c, a_c = dir_c[d]
        s_ctx, y_c = rwkv7_scan(s0, r_c, w_c, k_c, v_c, kk_c, a_c, rev)
        w_l, k_l, a_l = dir_l[d]
        _, y_l = rwkv7_scan(s_ctx, r_l, w_l, k_l, v_l, kk_l, a_l, rev)
        ys_l.append(y_l)
        ys_c.append(y_c)
        bon_l.append(jnp.sum(r_l * k_l * r_k, axis=-1, keepdims=True) * v_l)
        bon_c.append(jnp.sum(r_c * k_c * r_k, axis=-1, keepdims=True) * v_c)
    out_l = rwkv7_output(ys_l[0] + ys_l[1], bon_l[0] + bon_l[1], g_l, gn_g, gn_b)
    if not need_ctx:
        return out_l, None
    out_c = rwkv7_output(ys_c[0] + ys_c[1], bon_c[0] + bon_c[1], g_c, gn_g, gn_b)
    return out_l, out_c


def even_mixer(h_lat, h_ctx, cos, sin, w_in, w_out, qn_g, kn_g, mu, w0, w_up, a0, a_up, g_up,
               k_k, k_a, r_k, gn_g, gn_b, need_ctx):
    sizes = [A_Q, A_KV, A_KV, B_SHIFT]
    q_l, k_l, v_l, f_l = split_cols(h_lat @ w_in, sizes)
    q_c, k_c, v_c, f_c = split_cols(h_ctx @ w_in, sizes)
    q_l = apply_axial_rope(rms_norm(split_heads(q_l, A_HEADS), qn_g), cos, sin)
    k_l = apply_axial_rope(rms_norm(split_heads(k_l, A_KV_HEADS), kn_g), cos, sin)
    q_c = rms_norm(split_heads(q_c, A_HEADS), qn_g)
    k_c = rms_norm(split_heads(k_c, A_KV_HEADS), kn_g)
    v_l = split_heads(v_l, A_KV_HEADS)
    v_c = split_heads(v_c, A_KV_HEADS)
    grp = lambda q: q.reshape(*q.shape[:2], A_KV_HEADS, A_GROUP, A_HEAD_DIM)
    k_all = jnp.concatenate([k_c, k_l], axis=1)
    v_all = jnp.concatenate([v_c, v_l], axis=1)
    oa_l = blocked_gqa(grp(q_l), k_all, v_all)
    ob_l, ob_c = rwkv7_bidirectional(f_l, f_c, mu, w0, w_up, a0, a_up, g_up, k_k, k_a, r_k,
                                     gn_g, gn_b, need_ctx)
    o_lat = jnp.concatenate([oa_l, ob_l.astype(h_lat.dtype)], axis=-1) @ w_out
    if not need_ctx:
        return o_lat, None
    oa_c = gqa_attend(grp(q_c), k_c, v_c).reshape(*h_ctx.shape[:2], A_Q)
    o_ctx = jnp.concatenate([oa_c, ob_c.astype(h_ctx.dtype)], axis=-1) @ w_out
    return o_lat, o_ctx


def neighbourhood_attention(q, k, v, k_ctx, v_ctx, rpb):
    B_, L, H, hd = q.shape
    rows = L // GRID_W
    kh = min(NA_ROWS, rows)
    kw = NA_COLS
    n_keys = kh * GRID_W
    scale = hd ** -0.5
    qg = jnp.moveaxis(q.reshape(B_, rows, GRID_W, H, hd), 1, 0)
    kg = k.reshape(B_, rows, GRID_W, H, hd)
    vg = v.reshape(B_, rows, GRID_W, H, hd)
    col = jnp.arange(GRID_W)
    col_start = jnp.clip(col - kw // 2, 0, GRID_W - kw)
    key_col = jnp.tile(col, kh)
    key_row_off = jnp.repeat(jnp.arange(kh), GRID_W)
    in_window = ((key_col[None, :] >= col_start[:, None])
                 & (key_col[None, :] < col_start[:, None] + kw))
    dc_idx = jnp.clip(key_col[None, :] - col[:, None] + NA_COLS - 1, 0, 2 * NA_COLS - 2)

    def one_row(args):
        r, q_row = args
        r0 = jnp.clip(r - kh // 2, 0, rows - kh)
        k_blk = lax.dynamic_slice_in_dim(kg, r0, kh, axis=1).reshape(B_, n_keys, H, hd)
        v_blk = lax.dynamic_slice_in_dim(vg, r0, kh, axis=1).reshape(B_, n_keys, H, hd)
        dr_idx = jnp.clip(r0 + key_row_off - r + NA_ROWS - 1, 0, 2 * NA_ROWS - 2)
        bias = rpb[:, dr_idx[None, :], dc_idx].astype(jnp.float32)
        s_lat = jnp.einsum('bqhd,bkhd->bhqk', q_row, k_blk).astype(jnp.float32) * scale + bias
        s_lat = jnp.where(in_window, s_lat, NEG_INF)
        s_ctx = jnp.einsum('bqhd,bkhd->bhqk', q_row, k_ctx).astype(jnp.float32) * scale
        p = jax.nn.softmax(jnp.concatenate([s_lat, s_ctx], axis=-1), axis=-1).astype(v.dtype)
        return (jnp.einsum('bhqk,bkhd->bqhd', p[..., :n_keys], v_blk)
                + jnp.einsum('bhqk,bkhd->bqhd', p[..., n_keys:], v_ctx))

    o = lax.map(one_row, (jnp.arange(rows), qg))
    return jnp.moveaxis(o, 0, 1).reshape(B_, L, H * hd)


def odd_mixer(h_lat, h_ctx, w_in, w_out, qn_g, kn_g, rpb, need_ctx):
    q_l, k_l, v_l = split_cols(h_lat @ w_in, [C_WIDTH, C_WIDTH, C_WIDTH])
    if need_ctx:
        q_c, k_c, v_c = split_cols(h_ctx @ w_in, [C_WIDTH, C_WIDTH, C_WIDTH])
    else:
        k_c, v_c = split_cols(h_ctx @ w_in[:, C_WIDTH:], [C_WIDTH, C_WIDTH])
    q_l = rms_norm(split_heads(q_l, C_HEADS), qn_g)
    k_l = rms_norm(split_heads(k_l, C_HEADS), kn_g)
    k_c = rms_norm(split_heads(k_c, C_HEADS), kn_g)
    v_l = split_heads(v_l, C_HEADS)
    v_c = split_heads(v_c, C_HEADS)
    o_lat = neighbourhood_attention(q_l, k_l, v_l, k_c, v_c, rpb) @ w_out
    if not need_ctx:
        return o_lat, None
    q_c = rms_norm(split_heads(q_c, C_HEADS), qn_g)
    s = jnp.einsum('bqhd,bkhd->bhqk', q_c, k_c).astype(jnp.float32) * (C_HEAD_DIM ** -0.5)
    p = jax.nn.softmax(s, axis=-1).astype(v_c.dtype)
    o_c = jnp.einsum('bhqk,bkhd->bqhd', p, v_c)
    o_ctx = o_c.reshape(*h_ctx.shape[:2], C_WIDTH) @ w_out
    return o_lat, o_ctx


def grouped_moe(h, router_w, router_b, w_gate, w_up, w_down):
    T = h.shape[0]
    s = jax.nn.sigmoid((h @ router_w).astype(jnp.float32))
    sel = (s + router_b.astype(jnp.float32)).reshape(T, N_GROUPS, EXPERTS_PER_GROUP)
    top_val, top_idx = lax.top_k(sel, TOP_K)
    group = jnp.argmax(jnp.sum(top_val, axis=-1), axis=-1)
    local = jnp.take_along_axis(top_idx, group[:, None, None], axis=1)[:, 0]
    expert = group[:, None] * EXPERTS_PER_GROUP + local
    w = jnp.take_along_axis(s, expert, axis=1)
    w = w / jnp.sum(w, axis=-1, keepdims=True)
    gates = jnp.einsum('tke,tk->te', jax.nn.one_hot(expert, N_EXPERTS, dtype=jnp.float32), w).astype(h.dtype)
    out = jnp.zeros_like(h)
    for e in range(N_EXPERTS):
        y = (jax.nn.silu(h @ w_gate[e]) * (h @ w_up[e])) @ w_down[e]
        out = out + gates[:, e:e + 1] * y
    return out


def setup_inputs(seed: int = 0) -> dict:
    key = jax.random.key(seed)
    ks = jax.random.split(key, 40)
    counter = [0]

    def nxt():
        k = ks[counter[0]]
        counter[0] += 1
        return k

    def nrm(shape, scale):
        return scale * jax.random.normal(nxt(), shape, jnp.float32)

    def unif(shape, lo, hi):
        return jax.random.uniform(nxt(), shape, jnp.float32, lo, hi)

    D = D_MODEL
    inp = {}
    inp['x'] = nrm((BATCH, SEQ, D), 1.0)
    inp['c'] = nrm((BATCH, D), 1.0)
    inp['ctx'] = nrm((BATCH, CTX_LEN, D), 1.0)
    inp['c_ctx'] = nrm((D,), 1.0)
    inp['ada_w'] = nrm((DEPTH, D, N_MOD * D), 0.5 * D ** -0.5)
    inp['ada_b'] = nrm((DEPTH, N_MOD * D), 0.02)
    inp['norm_g'] = 1.0 + nrm((DEPTH, 2, D), 0.1)
    inp['ev_w_in'] = nrm((N_EVEN, D, EVEN_IN), D ** -0.5)
    inp['ev_w_out'] = nrm((N_EVEN, EVEN_MIX, D), EVEN_MIX ** -0.5)
    inp['ev_qn_g'] = 1.0 + nrm((N_EVEN, A_HEAD_DIM), 0.1)
    inp['ev_kn_g'] = 1.0 + nrm((N_EVEN, A_HEAD_DIM), 0.1)
    inp['ev_mu'] = unif((N_EVEN, 2, B_SHIFT), 0.0, 0.5)
    inp['ev_w0'] = unif((N_EVEN, 2, B_WIDTH), -6.0, -1.0)
    inp['ev_w_up'] = nrm((N_EVEN, 2, DECAY_LORA, B_WIDTH), 0.1)
    inp['ev_a0'] = nrm((N_EVEN, 2, B_WIDTH), 0.5)
    inp['ev_a_up'] = nrm((N_EVEN, 2, ICLR_LORA, B_WIDTH), 0.1)
    inp['ev_g_up'] = nrm((N_EVEN, GATE_LORA, B_WIDTH), GATE_LORA ** -0.5)
    inp['ev_k_k'] = 0.85 + nrm((N_EVEN, B_WIDTH), 0.1)
    inp['ev_k_a'] = 1.0 + nrm((N_EVEN, B_WIDTH), 0.1)
    inp['ev_r_k'] = nrm((N_EVEN, B_HEADS, B_HEAD_DIM), 0.1)
    inp['ev_gn_g'] = 1.0 + nrm((N_EVEN, B_WIDTH), 0.1)
    inp['ev_gn_b'] = nrm((N_EVEN, B_WIDTH), 0.02)
    inp['od_w_in'] = nrm((N_ODD, D, 3 * C_WIDTH), D ** -0.5)
    inp['od_w_out'] = nrm((N_ODD, C_WIDTH, D), C_WIDTH ** -0.5)
    inp['od_qn_g'] = 1.0 + nrm((N_ODD, C_HEAD_DIM), 0.1)
    inp['od_kn_g'] = 1.0 + nrm((N_ODD, C_HEAD_DIM), 0.1)
    inp['od_rpb'] = nrm((N_ODD, C_HEADS, 2 * NA_ROWS - 1, 2 * NA_COLS - 1), 0.1)
    inp['router_w'] = nrm((D, N_EXPERTS), D ** -0.5)
    inp['router_b'] = nrm((N_EXPERTS,), 0.01)
    inp['moe_w_gate'] = nrm((DEPTH, N_EXPERTS, D, D_EXPERT), D ** -0.5)
    inp['moe_w_up'] = nrm((DEPTH, N_EXPERTS, D, D_EXPERT), D ** -0.5)
    inp['moe_w_down'] = nrm((DEPTH, N_EXPERTS, D_EXPERT, D), D_EXPERT ** -0.5)
    return inp


def reference(x, c, ctx, c_ctx, ada_w, ada_b, norm_g, ev_w_in, ev_w_out, ev_qn_g, ev_kn_g, ev_mu,
              ev_w0, ev_w_up, ev_a0, ev_a_up, ev_g_up, ev_k_k, ev_k_a, ev_r_k, ev_gn_g, ev_gn_b,
              od_w_in, od_w_out, od_qn_g, od_kn_g, od_rpb, router_w, router_b,
              moe_w_gate, moe_w_up, moe_w_down):
    B_, L, D = x.shape
    cos, sin = axial_rope_tables(L, A_HEAD_DIM)
    silu_c = jax.nn.silu(c)
    silu_c_ctx = jax.nn.silu(c_ctx)
    x_lat, x_ctx = x, ctx
    for layer in range(DEPTH):
        need_ctx = layer < DEPTH - 1
        i = layer // 2
        mod_lat = (silu_c @ ada_w[layer] + ada_b[layer]).reshape(B_, N_MOD, 1, D)
        mod_ctx = (silu_c_ctx @ ada_w[layer] + ada_b[layer]).reshape(N_MOD, D)
        h_lat = rms_norm(x_lat, norm_g[layer, 0]) * (1.0 + mod_lat[:, 1]) + mod_lat[:, 0]
        h_ctx = rms_norm(x_ctx, norm_g[layer, 0]) * (1.0 + mod_ctx[1]) + mod_ctx[0]
        if layer % 2 == 0:
            o_lat, o_ctx = even_mixer(h_lat, h_ctx, cos, sin, ev_w_in[i], ev_w_out[i], ev_qn_g[i],
                                      ev_kn_g[i], ev_mu[i], ev_w0[i], ev_w_up[i], ev_a0[i],
                                      ev_a_up[i], ev_g_up[i], ev_k_k[i], ev_k_a[i], ev_r_k[i],
                                      ev_gn_g[i], ev_gn_b[i], need_ctx)
        else:
            o_lat, o_ctx = odd_mixer(h_lat, h_ctx, od_w_in[i], od_w_out[i], od_qn_g[i],
                                     od_kn_g[i], od_rpb[i], need_ctx)
        x_lat = x_lat + mod_lat[:, 2] * o_lat
        h_lat = rms_norm(x_lat, norm_g[layer, 1]) * (1.0 + mod_lat[:, 4]) + mod_lat[:, 3]
        tokens = h_lat.reshape(B_ * L, D)
        if need_ctx:
            x_ctx = x_ctx + mod_ctx[2] * o_ctx
            h_ctx = rms_norm(x_ctx, norm_g[layer, 1]) * (1.0 + mod_ctx[4]) + mod_ctx[3]
            tokens = jnp.concatenate([tokens, h_ctx.reshape(-1, D)], axis=0)
        f = grouped_moe(tokens, router_w, router_b, moe_w_gate[layer], moe_w_up[layer], moe_w_down[layer])
        x_lat = x_lat + mod_lat[:, 5] * f[:B_ * L].reshape(B_, L, D)
        if need_ctx:
            x_ctx = x_ctx + mod_ctx[5] * f[B_ * L:].reshape(x_ctx.shape)
    return x_lat
```

```python
import functools

import numpy as np
import jax
import jax.numpy as jnp
from jax import lax
from jax.experimental import pallas as pl
from jax.experimental.pallas import tpu as pltpu

f32 = jnp.float32
bf16 = jnp.bfloat16
HIGHEST = lax.Precision.HIGHEST

GRID_W = 64
N_MOD = 6
NORM_EPS = 1e-6
NEG_INF = -1e30
HEAD = 128
A_HEADS, A_KV_HEADS = 8, 2
A_Q, A_KV = A_HEADS * HEAD, A_KV_HEADS * HEAD
ROPE_THETA = 10000.0
B_HEADS, B_HEAD_DIM = 16, 64
B_WIDTH = B_HEADS * B_HEAD_DIM
DECAY_LORA, ICLR_LORA, GATE_LORA = 64, 64, 160
LORA_PAD = 512
GN_EPS = 64e-5
C_HEADS = 16
C_WIDTH = C_HEADS * HEAD
NA_ROWS, NA_COLS = 8, 16
N_EXPERTS, N_GROUPS, TOP_K = 16, 4, 2
EXPERTS_PER_GROUP = N_EXPERTS // N_GROUPS
CHUNK = 64
VMEM_LIMIT_BYTES = 50 * 1024 * 1024


def _cparams(n_axes):
    return pltpu.CompilerParams(dimension_semantics=("arbitrary",) * n_axes,
                                vmem_limit_bytes=VMEM_LIMIT_BYTES)


def _nt(a, b, precision=None):
    return lax.dot_general(a, b, (((1,), (1,)), ((), ())), preferred_element_type=f32, precision=precision)


def _tn(a, b, precision=None):
    return lax.dot_general(a, b, (((0,), (0,)), ((), ())), preferred_element_type=f32, precision=precision)


def _nn(a, b, precision=None):
    return jnp.dot(a, b, preferred_element_type=f32, precision=precision)


def _ada_kernel(c_ref, w_ref, b_ref, o_ref):
    c = c_ref[...]
    s = c * jax.nn.sigmoid(c)
    o_ref[0] = _nn(s, w_ref[0], HIGHEST) + b_ref[0]


def _ada_modulation(cvec, ada_w, ada_b):
    depth, d, n = ada_w.shape
    tn = 1024 if n % 1024 == 0 else 512
    return pl.pallas_call(
        _ada_kernel,
        grid=(depth, n // tn),
        in_specs=[pl.BlockSpec((8, d), lambda l, j: (0, 0)),
                  pl.BlockSpec((1, d, tn), lambda l, j: (l, 0, j)),
                  pl.BlockSpec((1, 1, tn), lambda l, j: (l, 0, j))],
        out_specs=pl.BlockSpec((1, 8, tn), lambda l, j: (l, 0, j)),
        out_shape=jax.ShapeDtypeStruct((depth, 8, n), f32),
        compiler_params=_cparams(2),
        name="ada_modulation",
    )(cvec, ada_w, ada_b.reshape(depth, 1, n))


def _norm_mod(x, g, scale, shift):
    ms = jnp.mean(x * x, axis=-1, keepdims=True)
    return (x * lax.rsqrt(ms + NORM_EPS) * g) * (1.0 + scale) + shift


def _nmm_kernel(x_ref, g_ref, sc_ref, sh_ref, w_ref, o_ref, h_ref):
    @pl.when(pl.program_id(1) == 0)
    def _():
        h_ref[...] = _norm_mod(x_ref[...], g_ref[...], sc_ref[0], sh_ref[0]).astype(bf16)

    o_ref[...] = _nn(h_ref[...], w_ref[...])


def _norm_mod_matmul(x, g, mod, shift_idx, scale_idx, w, seg_of_tile, tm):
    t, d = x.shape
    n = w.shape[1]
    tn = 512
    return pl.pallas_call(
        _nmm_kernel,
        grid=(t // tm, n // tn),
        in_specs=[pl.BlockSpec((tm, d), lambda i, j: (i, 0)),
                  pl.BlockSpec((1, d), lambda i, j: (0, 0)),
                  pl.BlockSpec((1, 1, d), lambda i, j: (seg_of_tile(i, tm) * N_MOD + scale_idx, 0, 0)),
                  pl.BlockSpec((1, 1, d), lambda i, j: (seg_of_tile(i, tm) * N_MOD + shift_idx, 0, 0)),
                  pl.BlockSpec((d, tn), lambda i, j: (0, j))],
        out_specs=pl.BlockSpec((tm, tn), lambda i, j: (i, j)),
        out_shape=jax.ShapeDtypeStruct((t, n), f32),
        scratch_shapes=[pltpu.VMEM((tm, d), bf16)],
        compiler_params=_cparams(2),
        name="norm_mod_matmul",
    )(x, g.reshape(1, d), mod, mod, w)


def _mmres_kernel(y_ref, w_ref, x_ref, gate_ref, o_ref):
    o_ref[...] = x_ref[...] + gate_ref[0] * _nn(y_ref[...], w_ref[...])


def _matmul_gated_residual(y, w, x, mod, gate_idx, seg_of_tile, tm):
    t, k = y.shape
    d = w.shape[1]
    tn = 512 if d % 512 == 0 else 256
    return pl.pallas_call(
        _mmres_kernel,
        grid=(t // tm, d // tn),
        in_specs=[pl.BlockSpec((tm, k), lambda i, j: (i, 0)),
                  pl.BlockSpec((k, tn), lambda i, j: (0, j)),
                  pl.BlockSpec((tm, tn), lambda i, j: (i, j)),
                  pl.BlockSpec((1, 1, tn), lambda i, j: (seg_of_tile(i, tm) * N_MOD + gate_idx, 0, j))],
        out_specs=pl.BlockSpec((tm, tn), lambda i, j: (i, j)),
        out_shape=jax.ShapeDtypeStruct((t, d), f32),
        compiler_params=_cparams(2),
        name="matmul_gated_residual",
    )(y, w, x, mod)


def _swap_pairs(x):
    lane = lax.broadcasted_iota(jnp.int32, x.shape, 1)
    first = (lane // 32) % 2 == 0
    return jnp.where(first, pltpu.roll(x, 96, axis=1), pltpu.roll(x, 32, axis=1))


def _qk_prep_kernel(*refs, n_q, n_k, rope):
    if rope:
        q_ref, k_ref, v_ref, qg_ref, kg_ref, cos_ref, sin_ref, qo_ref, ko_ref, vo_ref = refs
        cos, sin = cos_ref[...], sin_ref[...]
    else:
        q_ref, k_ref, v_ref, qg_ref, kg_ref, qo_ref, ko_ref, vo_ref = refs

    def head(src, dst, h, g, scale):
        xh = src[:, h * HEAD:(h + 1) * HEAD]
        ms = jnp.mean(xh * xh, axis=-1, keepdims=True)
        y = xh * lax.rsqrt(ms + NORM_EPS) * g
        if rope:
            y = y * cos + _swap_pairs(y) * sin
        if scale != 1.0:
            y = y * scale
        dst[:, h * HEAD:(h + 1) * HEAD] = y.astype(dst.dtype)

    qg, kg = qg_ref[...], kg_ref[...]
    for h in range(n_q):
        head(q_ref, qo_ref, h, qg, HEAD ** -0.5)
    for h in range(n_k):
        head(k_ref, ko_ref, h, kg, 1.0)
    vo_ref[...] = v_ref[...].astype(vo_ref.dtype)


def _qk_prep(p, q_spec, k_spec, v_spec, n_q, n_k, qg, kg, cos=None, sin=None, tm=256):
    t = p.shape[0]
    rope = cos is not None
    ins = [pl.BlockSpec((tm, q_spec[0]), lambda i: (i, q_spec[1])),
           pl.BlockSpec((tm, k_spec[0]), lambda i: (i, k_spec[1])),
           pl.BlockSpec((tm, v_spec[0]), lambda i: (i, v_spec[1])),
           pl.BlockSpec((1, HEAD), lambda i: (0, 0)),
           pl.BlockSpec((1, HEAD), lambda i: (0, 0))]
    args = [p, p, p, qg.reshape(1, HEAD), kg.reshape(1, HEAD)]
    if rope:
        ins += [pl.BlockSpec((tm, HEAD), lambda i: (i, 0))] * 2
        args += [cos, sin]
    return pl.pallas_call(
        functools.partial(_qk_prep_kernel, n_q=n_q, n_k=n_k, rope=rope),
        grid=(t // tm,),
        in_specs=ins,
        out_specs=[pl.BlockSpec((tm, q_spec[0]), lambda i: (i, 0)),
                   pl.BlockSpec((tm, k_spec[0]), lambda i: (i, 0)),
                   pl.BlockSpec((tm, v_spec[0]), lambda i: (i, 0))],
        out_shape=[jax.ShapeDtypeStruct((t, q_spec[0]), bf16),
                   jax.ShapeDtypeStruct((t, k_spec[0]), bf16),
                   jax.ShapeDtypeStruct((t, v_spec[0]), bf16)],
        compiler_params=_cparams(1),
        name="qk_prep_rope" if rope else "qk_prep",
    )(*args)


def _rope_tables(n_lat_tokens_per_batch, batch, n_ctx_tokens):
    t = jnp.arange(n_lat_tokens_per_batch)
    row = (t // GRID_W).astype(f32)
    col = (t % GRID_W).astype(f32)
    n_freq = HEAD // 4
    inv = ROPE_THETA ** (-jnp.arange(n_freq, dtype=f32) / n_freq)
    ang_r, ang_c = row[:, None] * inv, col[:, None] * inv
    cos = jnp.concatenate([jnp.cos(ang_r), jnp.cos(ang_r), jnp.cos(ang_c), jnp.cos(ang_c)], axis=1)
    sin = jnp.concatenate([-jnp.sin(ang_r), jnp.sin(ang_r), -jnp.sin(ang_c), jnp.sin(ang_c)], axis=1)
    cos = jnp.concatenate([jnp.tile(cos, (batch, 1)), jnp.ones((n_ctx_tokens, HEAD), f32)], axis=0)
    sin = jnp.concatenate([jnp.tile(sin, (batch, 1)), jnp.zeros((n_ctx_tokens, HEAD), f32)], axis=0)
    return cos, sin


def _softmax_update(s, v, m_ref, l_ref, acc_ref, g):
    m_prev = m_ref[g]
    m_new = jnp.maximum(m_prev, jnp.max(s, axis=-1, keepdims=True))
    alpha = jnp.exp(m_prev - m_new)
    p = jnp.exp(s - m_new)
    l_ref[g] = alpha * l_ref[g] + jnp.sum(p, axis=-1, keepdims=True)
    acc_ref[g] = alpha * acc_ref[g] + _nn(p.astype(bf16), v)
    m_ref[g] = m_new


def _flash_kernel(q_ref, kc_ref, vc_ref, k_ref, v_ref, o_ref, m_ref, l_ref, acc_ref, *, group):
    j = pl.program_id(3)

    @pl.when(j == 0)
    def _():
        m_ref[...] = jnp.full(m_ref.shape, NEG_INF, f32)
        l_ref[...] = jnp.zeros(l_ref.shape, f32)
        acc_ref[...] = jnp.zeros(acc_ref.shape, f32)
        kc, vc = kc_ref[...], vc_ref[...]
        for g in range(group):
            s = _nt(q_ref[:, g * HEAD:(g + 1) * HEAD], kc)
            _softmax_update(s, vc, m_ref, l_ref, acc_ref, g)

    k, v = k_ref[...], v_ref[...]
    for g in range(group):
        s = _nt(q_ref[:, g * HEAD:(g + 1) * HEAD], k)
        _softmax_update(s, v, m_ref, l_ref, acc_ref, g)

    @pl.when(j == pl.num_programs(3) - 1)
    def _():
        for g in range(group):
            o_ref[:, g * HEAD:(g + 1) * HEAD] = (acc_ref[g] / l_ref[g]).astype(o_ref.dtype)


def _latent_gqa(q, k, v, batch, n_lat, n_ctx, kv_heads, group):
    tq = 512 if n_lat % 512 == 0 else n_lat
    tk = 1024 if n_lat % 1024 == 0 else n_lat
    nq, nk = n_lat // tq, n_lat // tk
    ctx_blk0 = batch * n_lat // n_ctx
    gw = group * HEAD
    return pl.pallas_call(
        functools.partial(_flash_kernel, group=group),
        grid=(batch, kv_heads, nq, nk),
        in_specs=[pl.BlockSpec((tq, gw), lambda b, h, i, j: (b * nq + i, h)),
                  pl.BlockSpec((n_ctx, HEAD), lambda b, h, i, j: (ctx_blk0 + b, h)),
                  pl.BlockSpec((n_ctx, HEAD), lambda b, h, i, j: (ctx_blk0 + b, h)),
                  pl.BlockSpec((tk, HEAD), lambda b, h, i, j: (b * nk + j, h)),
                  pl.BlockSpec((tk, HEAD), lambda b, h, i, j: (b * nk + j, h))],
        out_specs=pl.BlockSpec((tq, gw), lambda b, h, i, j: (b * nq + i, h)),
        out_shape=jax.ShapeDtypeStruct((batch * n_lat, kv_heads * gw), bf16),
        scratch_shapes=[pltpu.VMEM((group, tq, 1), f32), pltpu.VMEM((group, tq, 1), f32),
                        pltpu.VMEM((group, tq, HEAD), f32)],
        compiler_params=_cparams(4),
        name="latent_gqa",
    )(q, k, v, k, v)


def _ctx_attn_kernel(q_ref, k_ref, v_ref, o_ref, *, group):
    k, v = k_ref[...], v_ref[...]
    for g in range(group):
        s = _nt(q_ref[:, g * HEAD:(g + 1) * HEAD], k)
        m = jnp.max(s, axis=-1, keepdims=True)
        p = jnp.exp(s - m)
        l = jnp.sum(p, axis=-1, keepdims=True)
        o_ref[:, g * HEAD:(g + 1) * HEAD] = (_nn(p.astype(bf16), v) / l).astype(o_ref.dtype)


def _context_self_attention(q, k, v, batch, n_lat, n_ctx, kv_heads, group):
    ctx_blk0 = batch * n_lat // n_ctx
    gw = group * HEAD
    return pl.pallas_call(
        functools.partial(_ctx_attn_kernel, group=group),
        grid=(batch, kv_heads),
        in_specs=[pl.BlockSpec((n_ctx, gw), lambda b, h: (ctx_blk0 + b, h)),
                  pl.BlockSpec((n_ctx, HEAD), lambda b, h: (ctx_blk0 + b, h)),
                  pl.BlockSpec((n_ctx, HEAD), lambda b, h: (ctx_blk0 + b, h))],
        out_specs=pl.BlockSpec((n_ctx, gw), lambda b, h: (b, h)),
        out_shape=jax.ShapeDtypeStruct((batch * n_ctx, kv_heads * gw), bf16),
        compiler_params=_cparams(2),
        name="context_self_attention",
    )(q, k, v)


def _natten_kernel(q_ref, kp_ref, kc_ref, kn_ref, vp_ref, vc_ref, vn_ref, kx_ref, vx_ref, bias_ref, o_ref,
                   kbuf, vbuf, *, rows):
    j = pl.program_id(2)
    blk = NA_ROWS * GRID_W
    for n, (kr, vr) in enumerate(((kp_ref, vp_ref), (kc_ref, vc_ref), (kn_ref, vn_ref))):
        kbuf[n * blk:(n + 1) * blk, :] = kr[...]
        vbuf[n * blk:(n + 1) * blk, :] = vr[...]
    kx, vx = kx_ref[...], vx_ref[...]

    def one_row(ri, carry):
        r = j * NA_ROWS + ri
        r0 = jnp.clip(r - NA_ROWS // 2, 0, rows - NA_ROWS)
        off = pl.multiple_of((NA_ROWS + r0 - j * NA_ROWS) * GRID_W, GRID_W)
        delta = r - r0
        q = q_ref[pl.ds(pl.multiple_of(ri * GRID_W, GRID_W), GRID_W), :]
        kk = kbuf[pl.ds(off, blk), :]
        vv = vbuf[pl.ds(off, blk), :]
        s = _nt(q, kk) + bias_ref[0, delta]
        sx = _nt(q, kx)
        m = jnp.maximum(jnp.max(s, axis=-1, keepdims=True), jnp.max(sx, axis=-1, keepdims=True))
        p, px = jnp.exp(s - m), jnp.exp(sx - m)
        l = jnp.sum(p, axis=-1, keepdims=True) + jnp.sum(px, axis=-1, keepdims=True)
        o = (_nn(p.astype(bf16), vv) + _nn(px.astype(bf16), vx)) / l
        o_ref[pl.ds(pl.multiple_of(ri * GRID_W, GRID_W), GRID_W), :] = o.astype(o_ref.dtype)
        return carry

    lax.fori_loop(0, NA_ROWS, one_row, 0)


def _natten_bias_table(rpb):
    col = jnp.arange(GRID_W)
    col_start = jnp.clip(col - NA_COLS // 2, 0, GRID_W - NA_COLS)
    key_col = jnp.tile(col, NA_ROWS)
    key_row = jnp.repeat(jnp.arange(NA_ROWS), GRID_W)
    in_window = (key_col[None, :] >= col_start[:, None]) & (key_col[None, :] < col_start[:, None] + NA_COLS)
    dc = jnp.clip(key_col[None, :] - col[:, None] + NA_COLS - 1, 0, 2 * NA_COLS - 2)
    delta = jnp.arange(NA_ROWS)
    dr = jnp.clip(key_row[None, :] - delta[:, None] + NA_ROWS - 1, 0, 2 * NA_ROWS - 2)
    tab = rpb[:, dr[:, None, :], dc[None, :, :]].astype(f32)
    return jnp.where(in_window[None, None], tab, NEG_INF)


def _neighbourhood_attention(q, k, v, bias, batch, n_lat, n_ctx, heads):
    rows = n_lat // GRID_W
    nblk = rows // NA_ROWS
    blk = NA_ROWS * GRID_W
    ctx_blk0 = batch * n_lat // n_ctx
    kv_spec = lambda shift: pl.BlockSpec(
        (blk, HEAD), lambda b, h, j: (b * nblk + jnp.clip(j + shift, 0, nblk - 1), h))
    return pl.pallas_call(
        functools.partial(_natten_kernel, rows=rows),
        grid=(batch, heads, nblk),
        in_specs=[pl.BlockSpec((blk, HEAD), lambda b, h, j: (b * nblk + j, h)),
                  kv_spec(-1), kv_spec(0), kv_spec(1), kv_spec(-1), kv_spec(0), kv_spec(1),
                  pl.BlockSpec((n_ctx, HEAD), lambda b, h, j: (ctx_blk0 + b, h)),
                  pl.BlockSpec((n_ctx, HEAD), lambda b, h, j: (ctx_blk0 + b, h)),
                  pl.BlockSpec((1, NA_ROWS, GRID_W, blk), lambda b, h, j: (h, 0, 0, 0))],
        out_specs=pl.BlockSpec((blk, HEAD), lambda b, h, j: (b * nblk + j, h)),
        out_shape=jax.ShapeDtypeStruct((batch * n_lat, heads * HEAD), bf16),
        scratch_shapes=[pltpu.VMEM((3 * blk, HEAD), bf16), pltpu.VMEM((3 * blk, HEAD), bf16)],
        compiler_params=_cparams(3),
        name="neighbourhood_attention",
    )(q, k, k, k, v, v, v, k, v, bias)


def _split_dot(x, e):
    hi = x.astype(bf16)
    lo = (x - hi.astype(f32)).astype(bf16)
    return _nn(hi, e) + _nn(lo, e)


def _rwkv_feat_kernel(start_ref, end_ref, pm_ref, pl_ref, pmp_ref, pmn_ref, plp_ref, pln_ref,
                      mum_ref, mul_ref, w0_ref, a0_ref, wup_ref, aup_ref, gup_ref, kk_ref, ka_ref, rk_ref,
                      e64_ref, r_o, v_o, kkn_o, g_o, bon_o, lw_o, kd_o, a_o):
    i = pl.program_id(0)
    tm = pm_ref.shape[0]
    keep_prev = 1.0 - start_ref[i].astype(f32)
    keep_next = 1.0 - end_ref[i].astype(f32)
    rowi = lax.broadcasted_iota(jnp.int32, (tm, 1), 0)

    def shifted(cur, prev_row, next_row, mu):
        up = jnp.where(rowi == 0, prev_row * keep_prev, pltpu.roll(cur, 1, axis=0))
        dn = jnp.where(rowi == tm - 1, next_row * keep_next, pltpu.roll(cur, tm - 1, axis=0))
        return cur + mu[0:1] * (up - cur) + mu[1:2] * (dn - cur)

    def main_cols(c):
        sl = slice(c * B_WIDTH, (c + 1) * B_WIDTH)
        return shifted(pm_ref[:, sl], pmp_ref[7:8, sl], pmn_ref[0:1, sl], mum_ref[:, sl])

    e64 = e64_ref[...]
    r = main_cols(0)
    k = main_cols(1)
    v = main_cols(2)
    lora = shifted(pl_ref[...], plp_ref[7:8, :], pln_ref[0:1, :], mul_ref[...])
    r_o[...] = r
    v_o[...] = v
    xwa = lora[:, 0:128]
    xg = lora[:, 128:384]
    g_o[...] = _nn(jax.nn.sigmoid(xg).astype(bf16), gup_ref[...])
    kk = k * kk_ref[...]
    kkn_o[...] = kk * lax.rsqrt(_split_dot(kk * kk, e64) + 1e-12)
    th = jnp.tanh(xwa).astype(bf16)
    xa = xwa.astype(bf16)
    rkr = r * rk_ref[...]
    bonus = jnp.zeros_like(r)
    for d in range(2):
        z = w0_ref[d:d + 1, :] + _nn(th, wup_ref[d])
        lw_o[d] = -float(np.exp(-0.5)) * jax.nn.sigmoid(z)
        a = jax.nn.sigmoid(a0_ref[d:d + 1, :] + _nn(xa, aup_ref[d]))
        a_o[d] = a
        kd = k * (1.0 + (a - 1.0) * ka_ref[...])
        kd_o[d] = kd
        bonus = bonus + _split_dot(rkr * kd, e64) * v
    bon_o[...] = bonus


def _rwkv_features(p, lora_blk, seg_start, seg_end, mu_main, mu_lora, w0, a0, w_up, a_up, g_up, k_k, k_a, r_k,
                   tm=256):
    t = p.shape[0]
    nt8 = t // 8
    w3 = 3 * B_WIDTH
    e64 = jnp.asarray(np.kron(np.eye(B_HEADS), np.ones((B_HEAD_DIM, B_HEAD_DIM))), bf16)
    full = lambda shape: pl.BlockSpec(shape, lambda i, s, e: (0,) * len(shape))
    row_out = pl.BlockSpec((tm, B_WIDTH), lambda i, s, e: (i, 0))
    dir_out = pl.BlockSpec((2, tm, B_WIDTH), lambda i, s, e: (0, i, 0))
    grid_spec = pltpu.PrefetchScalarGridSpec(
        num_scalar_prefetch=2,
        grid=(t // tm,),
        in_specs=[pl.BlockSpec((tm, w3), lambda i, s, e: (i, 0)),
                  pl.BlockSpec((tm, LORA_PAD), lambda i, s, e: (i, lora_blk)),
                  pl.BlockSpec((8, w3), lambda i, s, e: (jnp.maximum(i * (tm // 8) - 1, 0), 0)),
                  pl.BlockSpec((8, w3), lambda i, s, e: (jnp.minimum((i + 1) * (tm // 8), nt8 - 1), 0)),
                  pl.BlockSpec((8, LORA_PAD), lambda i, s, e: (jnp.maximum(i * (tm // 8) - 1, 0), lora_blk)),
                  pl.BlockSpec((8, LORA_PAD), lambda i, s, e: (jnp.minimum((i + 1) * (tm // 8), nt8 - 1), lora_blk)),
                  full((2, w3)), full((2, LORA_PAD)), full((2, B_WIDTH)), full((2, B_WIDTH)),
                  full((2, 128, B_WIDTH)), full((2, 128, B_WIDTH)), full((256, B_WIDTH)),
                  full((1, B_WIDTH)), full((1, B_WIDTH)), full((1, B_WIDTH)), full((B_WIDTH, B_WIDTH))],
        out_specs=[row_out] * 5 + [dir_out] * 3,
    )
    row_shape = jax.ShapeDtypeStruct((t, B_WIDTH), f32)
    dir_shape = jax.ShapeDtypeStruct((2, t, B_WIDTH), f32)
    return pl.pallas_call(
        _rwkv_feat_kernel,
        grid_spec=grid_spec,
        out_shape=[row_shape] * 5 + [dir_shape] * 3,
        compiler_params=_cparams(1),
        name="rwkv_features",
    )(seg_start, seg_end, p, p, p, p, p, p, mu_main, mu_lora, w0, a0, w_up, a_up, g_up,
      k_k.reshape(1, B_WIDTH), k_a.reshape(1, B_WIDTH), r_k.reshape(1, B_WIDTH), e64)


def _rwkv_scan_kernel(r_ref, v_ref, kk_ref, lw_ref, kd_ref, a_ref, y_ref, s_ref):
    d = pl.program_id(1)
    step = pl.program_id(2)
    c = CHUNK

    @pl.when(step == 0)
    def _():
        s_ref[...] = jnp.zeros(s_ref.shape, f32)

    row = lax.broadcasted_iota(jnp.int32, (c, c), 0)
    col = lax.broadcasted_iota(jnp.int32, (c, c), 1)
    order = jnp.where(d == 0, col - row, row - col)
    strict = order < 0
    incl = order <= 0
    lw_all = lw_ref[...]
    cum_all = _nn(incl.astype(f32), lw_all, HIGHEST)
    tot_all = jnp.sum(lw_all, axis=0, keepdims=True)

    for h in range(B_HEADS):
        sl = slice(h * B_HEAD_DIM, (h + 1) * B_HEAD_DIM)
        lw, cum, tot = lw_all[:, sl], cum_all[:, sl], tot_all[:, sl]
        r, v, kk, kd, a = r_ref[:, sl], v_ref[:, sl], kk_ref[:, sl], kd_ref[:, sl], a_ref[:, sl]
        e_neg = jnp.exp(-cum)
        e_end = jnp.exp(tot - cum)
        b = kk * a
        at = (-kk * jnp.exp(cum - lw)).astype(bf16)
        rt = (r * jnp.exp(cum)).astype(bf16)
        la = jnp.concatenate([at, rt], axis=0)
        gb = _nt(la, (b * e_neg).astype(bf16))
        gk = _nt(la, (kd * e_neg).astype(bf16))
        a_ab = jnp.where(strict, gb[:c], 0.0).astype(bf16)
        a_ak = jnp.where(strict, gk[:c], 0.0).astype(bf16)
        a_rb = jnp.where(incl, gb[c:], 0.0).astype(bf16)
        a_rk = jnp.where(incl, gk[c:], 0.0).astype(bf16)
        s0 = s_ref[h]
        ls = _nt(la, s0.astype(bf16))
        vb = v.astype(bf16)
        u = ls[:c] + _nn(a_ak, vb)
        npow = a_ab
        n_steps = int(np.log2(c))
        for it in range(n_steps):
            u = u + _nn(npow, u.astype(bf16))
            if it + 1 < n_steps:
                npow = _nn(npow, npow).astype(bf16)
        ub = u.astype(bf16)
        y_ref[:, sl] = ls[c:] + _nn(a_rb, ub) + _nn(a_rk, vb)
        s_ref[h] = s0 * jnp.exp(tot) + _tn(ub, (b * e_end).astype(bf16)) + _tn(vb, (kd * e_end).astype(bf16))


def _rwkv_scan(r, v, kk, lw, kd, a, batch, n_lat, n_ctx):
    t = r.shape[0]
    nc_ctx, nc_lat = n_ctx // CHUNK, n_lat // CHUNK
    lat_blocks = batch * nc_lat

    def blk(b, d, s):
        fwd = jnp.where(s < nc_ctx, lat_blocks + b * nc_ctx + s, b * nc_lat + (s - nc_ctx))
        bwd = jnp.where(s < nc_ctx, lat_blocks + b * nc_ctx + (nc_ctx - 1 - s),
                        b * nc_lat + (nc_lat - 1 - (s - nc_ctx)))
        return jnp.where(d == 0, fwd, bwd)

    shared = pl.BlockSpec((CHUNK, B_WIDTH), lambda b, d, s: (blk(b, d, s), 0))
    per_dir = pl.BlockSpec((None, CHUNK, B_WIDTH), lambda b, d, s: (d, blk(b, d, s), 0))
    return pl.pallas_call(
        _rwkv_scan_kernel,
        grid=(batch, 2, nc_ctx + nc_lat),
        in_specs=[shared, shared, shared, per_dir, per_dir, per_dir],
        out_specs=per_dir,
        out_shape=jax.ShapeDtypeStruct((2, t, B_WIDTH), f32),
        scratch_shapes=[pltpu.VMEM((B_HEADS, B_HEAD_DIM, B_HEAD_DIM), f32)],
        compiler_params=_cparams(3),
        name="rwkv_scan",
    )(r, v, kk, lw, kd, a)


def _rwkv_out_kernel(y_ref, bon_ref, g_ref, gng_ref, gnb_ref, e64_ref, o_ref):
    e64 = e64_ref[...]
    y = y_ref[0] + y_ref[1]
    mean = _split_dot(y, e64) * (1.0 / B_HEAD_DIM)
    yc = y - mean
    var = _split_dot(yc * yc, e64) * (1.0 / B_HEAD_DIM)
    yn = yc * lax.rsqrt(var + GN_EPS) * gng_ref[...] + gnb_ref[...]
    o_ref[...] = ((yn + bon_ref[...]) * g_ref[...]).astype(o_ref.dtype)


def _rwkv_output(y, bonus, g, gn_g, gn_b, tm=256):
    t = bonus.shape[0]
    e64 = jnp.asarray(np.kron(np.eye(B_HEADS), np.ones((B_HEAD_DIM, B_HEAD_DIM))), bf16)
    row = pl.BlockSpec((tm, B_WIDTH), lambda i: (i, 0))
    vec = pl.BlockSpec((1, B_WIDTH), lambda i: (0, 0))
    return pl.pallas_call(
        _rwkv_out_kernel,
        grid=(t // tm,),
        in_specs=[pl.BlockSpec((2, tm, B_WIDTH), lambda i: (0, i, 0)), row, row, vec, vec,
                  pl.BlockSpec((B_WIDTH, B_WIDTH), lambda i: (0, 0))],
        out_specs=row,
        out_shape=jax.ShapeDtypeStruct((t, B_WIDTH), bf16),
        compiler_params=_cparams(1),
        name="rwkv_output",
    )(y, bonus, g, gn_g.reshape(1, B_WIDTH), gn_b.reshape(1, B_WIDTH), e64)


def _router_kernel(x_ref, g_ref, sc_ref, sh_ref, rw_ref, rb_ref, h_ref, gates_ref):
    h = _norm_mod(x_ref[...], g_ref[...], sc_ref[0], sh_ref[0])
    h_ref[...] = h.astype(h_ref.dtype)
    s = jax.nn.sigmoid(_nt(rw_ref[...], h, HIGHEST))
    sel = s + rb_ref[...]
    rows = [sel[e:e + 1, :] for e in range(N_EXPERTS)]
    in_top = []
    score = []
    for grp in range(N_GROUPS):
        members = range(grp * EXPERTS_PER_GROUP, (grp + 1) * EXPERTS_PER_GROUP)
        acc = jnp.zeros_like(rows[0])
        for e in members:
            rank = jnp.zeros_like(rows[0])
            for o in members:
                if o == e:
                    continue
                ahead = (rows[o] >= rows[e]) if o < e else (rows[o] > rows[e])
                rank = rank + ahead.astype(f32)
            top = rank < TOP_K
            in_top.append(top)
            acc = acc + jnp.where(top, rows[e], 0.0)
        score.append(acc)
    group_on = []
    for grp in range(N_GROUPS):
        chosen = None
        for o in range(N_GROUPS):
            if o == grp:
                continue
            beaten = (score[o] >= score[grp]) if o < grp else (score[o] > score[grp])
            chosen = ~beaten if chosen is None else (chosen & ~beaten)
        group_on.append(chosen)
    picked = [in_top[e] & group_on[e // EXPERTS_PER_GROUP] for e in range(N_EXPERTS)]
    weight = [jnp.where(picked[e], s[e:e + 1, :], 0.0) for e in range(N_EXPERTS)]
    denom = weight[0]
    for e in range(1, N_EXPERTS):
        denom = denom + weight[e]
    for e in range(N_EXPERTS):
        gates_ref[e:e + 1, :] = weight[e] / denom


def _router(x, g, mod, shift_idx, scale_idx, router_w, router_b, seg_of_tile, tm):
    t, d = x.shape
    return pl.pallas_call(
        _router_kernel,
        grid=(t // tm,),
        in_specs=[pl.BlockSpec((tm, d), lambda i: (i, 0)),
                  pl.BlockSpec((1, d), lambda i: (0, 0)),
                  pl.BlockSpec((1, 1, d), lambda i: (seg_of_tile(i, tm) * N_MOD + scale_idx, 0, 0)),
                  pl.BlockSpec((1, 1, d), lambda i: (seg_of_tile(i, tm) * N_MOD + shift_idx, 0, 0)),
                  pl.BlockSpec((N_EXPERTS, d), lambda i: (0, 0)),
                  pl.BlockSpec((N_EXPERTS, 1), lambda i: (0, 0))],
        out_specs=[pl.BlockSpec((tm, d), lambda i: (i, 0)),
                   pl.BlockSpec((N_EXPERTS, tm), lambda i: (0, i))],
        out_shape=[jax.ShapeDtypeStruct((t, d), bf16), jax.ShapeDtypeStruct((N_EXPERTS, t), f32)],
        compiler_params=_cparams(1),
        name="router",
    )(x, g.reshape(1, d), mod, mod, router_w.T, router_b.reshape(N_EXPERTS, 1))


def _moe_dense_kernel(h_ref, gates_ref, wg_ref, wu_ref, wd_ref, x_ref, gate_ref, o_ref, acc_ref):
    e = pl.program_id(1)
    f = pl.program_id(2)

    @pl.when((e == 0) & (f == 0))
    def _():
        acc_ref[...] = jnp.zeros(acc_ref.shape, f32)

    gates = gates_ref[...]
    lane = lax.broadcasted_iota(jnp.int32, gates.shape, 1)
    gcol = jnp.sum(jnp.where(lane == e, gates, 0.0), axis=-1, keepdims=True)
    h = h_ref[...]
    a = _nn(h, wg_ref[0])
    u = _nn(h, wu_ref[0])
    act = (a * jax.nn.sigmoid(a)) * u * gcol
    acc_ref[...] += _nn(act.astype(bf16), wd_ref[0])

    @pl.when((e == pl.num_programs(1) - 1) & (f == pl.num_programs(2) - 1))
    def _():
        o_ref[...] = x_ref[...] + gate_ref[0] * acc_ref[...]


def _moe_dense(h, gates, w_gate, w_up, w_down, x, mod, gate_idx, seg_of_tile, tm):
    t, d = h.shape
    n_e, _, fdim = w_gate.shape
    nf = 2 if fdim % 256 == 0 else 1
    fh = fdim // nf
    return pl.pallas_call(
        _moe_dense_kernel,
        grid=(t // tm, n_e, nf),
        in_specs=[pl.BlockSpec((tm, d), lambda i, e, f: (i, 0)),
                  pl.BlockSpec((tm, n_e), lambda i, e, f: (i, 0)),
                  pl.BlockSpec((1, d, fh), lambda i, e, f: (e, 0, f)),
                  pl.BlockSpec((1, d, fh), lambda i, e, f: (e, 0, f)),
                  pl.BlockSpec((1, fh, d), lambda i, e, f: (e, f, 0)),
                  pl.BlockSpec((tm, d), lambda i, e, f: (i, 0)),
                  pl.BlockSpec((1, 1, d), lambda i, e, f: (seg_of_tile(i, tm) * N_MOD + gate_idx, 0, 0))],
        out_specs=pl.BlockSpec((tm, d), lambda i, e, f: (i, 0)),
        out_shape=jax.ShapeDtypeStruct((t, d), f32),
        scratch_shapes=[pltpu.VMEM((tm, d), f32)],
        compiler_params=_cparams(3),
        name="moe_dense",
    )(h, gates, w_gate, w_up, w_down, x, mod)


def _pad_rows(w, rows):
    return jnp.concatenate([w, jnp.zeros((rows - w.shape[0],) + w.shape[1:], w.dtype)], axis=0)


def kernel(x, c, ctx, c_ctx, ada_w, ada_b, norm_g, ev_w_in, ev_w_out, ev_qn_g, ev_kn_g, ev_mu, ev_w0, ev_w_up,
           ev_a0, ev_a_up, ev_g_up, ev_k_k, ev_k_a, ev_r_k, ev_gn_g, ev_gn_b, od_w_in, od_w_out, od_qn_g,
           od_kn_g, od_rpb, router_w, router_b, moe_w_gate, moe_w_up, moe_w_down):
    batch, n_lat, d = x.shape
    n_ctx = ctx.shape[1]
    depth = ada_w.shape[0]
    t_lat, t_ctx = batch * n_lat, batch * n_ctx
    t = t_lat + t_ctx
    assert n_lat % (NA_ROWS * GRID_W) == 0 and n_lat % n_ctx == 0 and n_ctx % 256 == 0
    assert batch + 1 <= 8
    tm = 512

    def seg_of_tile(i, tile):
        return jnp.minimum((i * tile) // n_lat, batch)

    xs = jnp.concatenate([x.reshape(t_lat, d), ctx.reshape(t_ctx, d)], axis=0)

    cvec = jnp.concatenate([c, c_ctx[None, :], jnp.zeros((8 - batch - 1, d), f32)], axis=0)
    mod_all = _ada_modulation(cvec, ada_w, ada_b)
    mod_all = mod_all[:, :batch + 1].reshape(depth, (batch + 1) * N_MOD, 1, d)

    cos, sin = _rope_tables(n_lat, batch, t_ctx)
    tiles = np.arange(t // 256) * 256
    seg_len = np.where(tiles < t_lat, n_lat, n_ctx)
    seg_off = np.where(tiles < t_lat, tiles, tiles - t_lat)
    seg_start = jnp.asarray((seg_off % seg_len == 0).astype(np.int32))
    seg_end = jnp.asarray(((seg_off + 256) % seg_len == 0).astype(np.int32))

    for layer in range(depth):
        i = layer // 2
        mod = mod_all[layer]
        if layer % 2 == 0:
            w = ev_w_in[i]
            fcol = A_Q + 2 * A_KV
            n_lora = DECAY_LORA + ICLR_LORA + GATE_LORA
            w_in = jnp.concatenate(
                [w[:, fcol:fcol + 3 * B_WIDTH], w[:, :fcol], w[:, fcol + 3 * B_WIDTH:],
                 jnp.zeros((d, LORA_PAD - n_lora), f32)], axis=1).astype(bf16)
            p = _norm_mod_matmul(xs, norm_g[layer, 0], mod, 0, 1, w_in, seg_of_tile, tm)
            q, k, v = _qk_prep(p, (A_Q, 3 * B_WIDTH // A_Q), (A_KV, (3 * B_WIDTH + A_Q) // A_KV),
                               (A_KV, (3 * B_WIDTH + A_Q) // A_KV + 1), A_HEADS, A_KV_HEADS,
                               ev_qn_g[i], ev_kn_g[i], cos, sin)
            group = A_HEADS // A_KV_HEADS
            oa_lat = _latent_gqa(q, k, v, batch, n_lat, n_ctx, A_KV_HEADS, group)
            oa_ctx = _context_self_attention(q, k, v, batch, n_lat, n_ctx, A_KV_HEADS, group)
            mu = ev_mu[i]
            mu_lora = jnp.concatenate([mu[:, 3 * B_WIDTH:], jnp.zeros((2, LORA_PAD - n_lora), f32)], axis=1)
            w_up = jnp.concatenate([ev_w_up[i], jnp.zeros((2, 128 - DECAY_LORA, B_WIDTH), f32)], axis=1)
            a_up = jnp.concatenate([jnp.zeros((2, DECAY_LORA, B_WIDTH), f32), ev_a_up[i]], axis=1)
            g_up = _pad_rows(ev_g_up[i], 256)
            lora_blk = (3 * B_WIDTH + A_Q + 2 * A_KV) // LORA_PAD
            r_, v_, kk_, g_, bon_, lw_, kd_, a_ = _rwkv_features(
                p, lora_blk, seg_start, seg_end, mu[:, :3 * B_WIDTH], mu_lora, ev_w0[i], ev_a0[i],
                w_up.astype(bf16), a_up.astype(bf16), g_up.astype(bf16), ev_k_k[i], ev_k_a[i], ev_r_k[i])
            y = _rwkv_scan(r_, v_, kk_, lw_, kd_, a_, batch, n_lat, n_ctx)
            ob = _rwkv_output(y, bon_, g_, ev_gn_g[i], ev_gn_b[i])
            mix = jnp.concatenate([jnp.concatenate([oa_lat, oa_ctx], axis=0), ob], axis=1)
            xs = _matmul_gated_residual(mix, ev_w_out[i].astype(bf16), xs, mod, 2, seg_of_tile, tm)
        else:
            p = _norm_mod_matmul(xs, norm_g[layer, 0], mod, 0, 1, od_w_in[i].astype(bf16), seg_of_tile, tm)
            q, k, v = _qk_prep(p, (C_WIDTH, 0), (C_WIDTH, 1), (C_WIDTH, 2), C_HEADS, C_HEADS,
                               od_qn_g[i], od_kn_g[i])
            bias = _natten_bias_table(od_rpb[i])
            o_lat = _neighbourhood_attention(q, k, v, bias, batch, n_lat, n_ctx, C_HEADS)
            o_ctx = _context_self_attention(q, k, v, batch, n_lat, n_ctx, C_HEADS, 1)
            mix = jnp.concatenate([o_lat, o_ctx], axis=0)
            xs = _matmul_gated_residual(mix, od_w_out[i].astype(bf16), xs, mod, 2, seg_of_tile, tm)
        h, gates_t = _router(xs, norm_g[layer, 1], mod, 3, 4, router_w, router_b, seg_of_tile, tm)
        xs = _moe_dense(h, gates_t.T, moe_w_gate[layer].astype(bf16), moe_w_up[layer].astype(bf16),
                        moe_w_down[layer].astype(bf16), xs, mod, 5, seg_of_tile, tm)
    return xs[:t_lat].reshape(batch, n_lat, d)
```
